```python
import math
import jax, jax.numpy as jnp
from jax import lax
import numpy as np

D_MODEL = 1024
BATCH = 8
SEQ = 2048
DEPTH = 2

SSD_HEADS = 16
SSD_HEAD_DIM = 64
SSD_INNER = SSD_HEADS * SSD_HEAD_DIM
SSD_GROUPS = 2
SSD_STATE = 128
SSD_CONV = 4
SSD_CHUNK = 64
GDN_HEADS = 4
GDN_HEAD_DIM = 128
GDN_WIDTH = GDN_HEADS * GDN_HEAD_DIM
GDN_CONV = 4
GDN_CHUNK = 64
GLA_HEADS = 4
GLA_KEY_DIM = 64
GLA_VAL_DIM = 128
GLA_K = GLA_HEADS * GLA_KEY_DIM
GLA_V = GLA_HEADS * GLA_VAL_DIM
GLA_GATE_RANK = 16
GLA_GATE_NORMALIZER = 16.0
GLA_CHUNK = 16
FFN_DIM = 2816
FFN_CONV = 3
N_BRANCH = 3
ALPHA = (2 * DEPTH) ** 0.25
BETA = (8 * DEPTH) ** -0.25
LN_EPS = 1e-5
RMS_EPS = 1e-6
DT_MIN = 0.001
DT_MAX = 0.1

SPLITS = (
    SSD_INNER,
    SSD_INNER + 2 * SSD_GROUPS * SSD_STATE,
    SSD_HEADS,
    3 * GDN_WIDTH,
    GDN_HEADS,
    GDN_HEADS,
    GDN_WIDTH,
    2 * GLA_K + GLA_V,
    GLA_GATE_RANK,
    GLA_V,
    N_BRANCH * D_MODEL,
)
IN_DIM = int(sum(SPLITS))
SPLIT_IDX = tuple(int(i) for i in np.cumsum(SPLITS)[:-1])

kernel_name = "hybrid_ssd_gdn_gla_deepnorm"


def causal_dwconv(x, w, b=None):
    K = w.shape[0]
    T = x.shape[1]
    xp = jnp.pad(x, ((0, 0), (K - 1, 0), (0, 0)))
    y = xp[:, 0:T] * w[0]
    for k in range(1, K):
        y = y + xp[:, k:k + T] * w[k]
    if b is not None:
        y = y + b
    return y


def layer_norm(x, g, b):
    xf = x.astype(jnp.float32)
    mu = jnp.mean(xf, -1, keepdims=True)
    var = jnp.mean(jnp.square(xf - mu), -1, keepdims=True)
    return ((xf - mu) * lax.rsqrt(var + LN_EPS) * g + b).astype(x.dtype)


def rms_norm(x, w):
    return x * lax.rsqrt(jnp.mean(x * x, -1, keepdims=True) + RMS_EPS) * w


def l2norm(x):
    return x * lax.rsqrt(jnp.sum(x * x, -1, keepdims=True) + RMS_EPS)


def ssd_mixer(z, xbc, dt_raw, conv_w, conv_b, dt_bias, a_log, d_skip, norm_w):
    f32 = jnp.float32
    Bsz, T, _ = z.shape
    L = SSD_CHUNK
    nc = T // L
    G = SSD_GROUPS
    R = SSD_HEADS // G
    P = SSD_HEAD_DIM
    N = SSD_STATE
    xbc = jax.nn.silu(causal_dwconv(xbc, conv_w, conv_b)).astype(f32)
    xs, Bm, Cm = jnp.split(xbc, [SSD_INNER, SSD_INNER + G * N], axis=-1)
    xs = xs.reshape(Bsz, nc, L, G, R, P)
    Bm = Bm.reshape(Bsz, nc, L, G, N)
    Cm = Cm.reshape(Bsz, nc, L, G, N)
    dt = jax.nn.softplus(dt_raw.astype(f32) + dt_bias.astype(f32)).reshape(Bsz, nc, L, G, R)
    A = -jnp.exp(a_log.astype(f32)).reshape(G, R)
    a_cs = jnp.cumsum(dt * A, axis=2)
    causal = jnp.tril(jnp.ones((L, L), bool))[:, :, None, None]
    seg = a_cs[:, :, :, None] - a_cs[:, :, None]
    decay = jnp.exp(jnp.where(causal, seg, -jnp.inf))
    CB = jnp.einsum('bclgn,bcsgn->bclsg', Cm, Bm)
    W = CB[..., None] * decay * dt[:, :, None]
    y = jnp.einsum('bclsgr,bcsgrp->bclgrp', W, xs)
    dec_end = jnp.exp(a_cs[:, :, -1:] - a_cs)
    states = jnp.einsum('bclgn,bclgr,bclgrp->bcgrpn', Bm, dec_end * dt, xs)
    chunk_decay = jnp.exp(a_cs[:, :, -1])

    def step(S, inp):
        st, cd = inp
        return S * cd[..., None, None] + st, S

    S0 = jnp.zeros((Bsz, G, R, P, N), f32)
    _, S_prev = lax.scan(step, S0, (jnp.moveaxis(states, 1, 0), jnp.moveaxis(chunk_decay, 1, 0)))
    S_prev = jnp.moveaxis(S_prev, 0, 1)
    y = y + jnp.einsum('bclgn,bcgrpn,bclgr->bclgrp', Cm, S_prev, jnp.exp(a_cs))
    y = y + d_skip.astype(f32).reshape(G, R)[:, :, None] * xs
    y = y.reshape(Bsz, T, SSD_INNER) * jax.nn.silu(z.astype(f32))
    y = rms_norm(y.reshape(Bsz, T, G, SSD_INNER // G), 1.0).reshape(Bsz, T, SSD_INNER)
    return (y * norm_w).astype(z.dtype)


def gdn_mixer(qkv, a_raw, b_raw, gate, conv_w, a_log, dt_bias, norm_w):
    f32 = jnp.float32
    Bsz, T, _ = qkv.shape
    H, Dh, L = GDN_HEADS, GDN_HEAD_DIM, GDN_CHUNK
    nc = T // L
    qkv = jax.nn.silu(causal_dwconv(qkv, conv_w)).astype(f32)
    q, k, v = jnp.split(qkv, 3, axis=-1)

    def heads(t):
        return t.reshape(Bsz, nc, L, H, Dh).transpose(0, 1, 3, 2, 4)

    def per_head(t):
        return t.reshape(Bsz, nc, L, H).transpose(0, 1, 3, 2)

    q = l2norm(heads(q)) * (Dh ** -0.5)
    k = l2norm(heads(k))
    v = heads(v)
    beta = per_head(jax.nn.sigmoid(b_raw.astype(f32)))
    g = -jnp.exp(a_log.astype(f32)) * jax.nn.softplus(a_raw.astype(f32) + dt_bias.astype(f32))
    g_cs = jnp.cumsum(per_head(g), -1)
    incl = jnp.tril(jnp.ones((L, L), bool))
    strict = jnp.tril(jnp.ones((L, L), bool), -1)
    gamma = jnp.exp(jnp.where(incl, g_cs[..., :, None] - g_cs[..., None, :], -jnp.inf))
    kb = k * beta[..., None]
    A = jnp.where(strict, jnp.einsum('bchld,bchsd->bchls', kb, k) * gamma, 0.0)
    eye = jnp.eye(L, dtype=f32)
    Tm = lax.linalg.triangular_solve(eye + A, jnp.broadcast_to(eye, A.shape),
                                     left_side=True, lower=True)
    u = jnp.einsum('bchls,bchsd->bchld', Tm, v * beta[..., None])
    w = jnp.einsum('bchls,bchsd->bchld', Tm, kb * jnp.exp(g_cs)[..., None])
    attn = jnp.where(incl, jnp.einsum('bchld,bchsd->bchls', q, k) * gamma, 0.0)
    q_dec = q * jnp.exp(g_cs)[..., None]
    k_dec = k * jnp.exp(g_cs[..., -1:] - g_cs)[..., None]
    chunk_decay = jnp.exp(g_cs[..., -1])

    def step(S, inp):
        u_c, w_c, qd_c, kd_c, a_c, cd_c = inp
        v_new = u_c - jnp.einsum('bhld,bhdv->bhlv', w_c, S)
        o = jnp.einsum('bhld,bhdv->bhlv', qd_c, S) + jnp.einsum('bhls,bhsv->bhlv', a_c, v_new)
        S = S * cd_c[..., None, None] + jnp.einsum('bhld,bhlv->bhdv', kd_c, v_new)
        return S, o

    seq_in = tuple(jnp.moveaxis(t, 1, 0) for t in (u, w, q_dec, k_dec, attn, chunk_decay))
    _, o = lax.scan(step, jnp.zeros((Bsz, H, Dh, Dh), f32), seq_in)
    o = o.transpose(1, 0, 3, 2, 4).reshape(Bsz, T, H, Dh)
    o = rms_norm(o, norm_w.astype(f32)) * jax.nn.silu(gate.astype(f32).reshape(Bsz, T, H, Dh))
    return o.reshape(Bsz, T, GDN_WIDTH).astype(gate.dtype)


def gla_mixer(qkv, g_lr, r, gate_w2, gate_b, norm_w):
    f32 = jnp.float32
    Bsz, T, _ = qkv.shape
    H, K, V, L = GLA_HEADS, GLA_KEY_DIM, GLA_VAL_DIM, GLA_CHUNK
    nc = T // L
    q, k, v = jnp.split(qkv.astype(f32), [GLA_K, 2 * GLA_K], axis=-1)
    gk = jax.nn.log_sigmoid(g_lr.astype(f32) @ gate_w2.astype(f32) + gate_b) / GLA_GATE_NORMALIZER
    q = q.reshape(Bsz, nc, L, H, K) * (K ** -0.5)
    k = k.reshape(Bsz, nc, L, H, K)
    v = v.reshape(Bsz, nc, L, H, V)
    b_cs = jnp.cumsum(gk.reshape(Bsz, nc, L, H, K), axis=2)
    q_e = q * jnp.exp(b_cs)
    k_e = k * jnp.exp(-b_cs)
    incl = jnp.tril(jnp.ones((L, L), bool))
    A = jnp.where(incl, jnp.einsum('bclhk,bcshk->bchls', q_e, k_e), 0.0)
    o = jnp.einsum('bchls,bcshv->bclhv', A, v)
    k_d = k * jnp.exp(b_cs[:, :, -1:] - b_cs)
    states = jnp.einsum('bclhk,bclhv->bchkv', k_d, v)
    chunk_decay = jnp.exp(b_cs[:, :, -1])

    def step(S, inp):
        st, cd = inp
        return S * cd[..., None] + st, S

    _, S_prev = lax.scan(step, jnp.zeros((Bsz, H, K, V), f32),
                         (jnp.moveaxis(states, 1, 0), jnp.moveaxis(chunk_decay, 1, 0)))
    S_prev = jnp.moveaxis(S_prev, 0, 1)
    o = o + jnp.einsum('bclhk,bchkv->bclhv', q_e, S_prev)
    o = o.reshape(Bsz, T, H, V)
    o = rms_norm(o, norm_w.astype(f32)) * jax.nn.silu(r.astype(f32).reshape(Bsz, T, H, V))
    return o.reshape(Bsz, T, GLA_V).astype(r.dtype)


def conv_glu_ffn(x, w_up, conv_w, conv_b, w_down):
    h = causal_dwconv(x @ w_up, conv_w, conv_b)
    g, u = jnp.split(h, 2, axis=-1)
    return (jax.nn.silu(g) * u) @ w_down


def setup_inputs(seed: int = 0) -> dict:
    key = jax.random.key(seed)
    keys = jax.random.split(key, 32)
    counter = iter(range(32))

    def nrm(shape, scale):
        return jax.random.normal(keys[next(counter)], shape, jnp.float32) * scale

    def unif(shape, lo, hi):
        return jax.random.uniform(keys[next(counter)], shape, jnp.float32, lo, hi)

    def dt_bias(n):
        dt = jnp.exp(unif((DEPTH, n), math.log(DT_MIN), math.log(DT_MAX)))
        return dt + jnp.log(-jnp.expm1(-dt))

    Dm = D_MODEL
    ssd_conv_ch = SSD_INNER + 2 * SSD_GROUPS * SSD_STATE
    return {
        "x": nrm((BATCH, SEQ, Dm), 1.0),
        "w_in": nrm((DEPTH, Dm, IN_DIM), Dm ** -0.5),
        "ssd_conv_w": nrm((DEPTH, SSD_CONV, ssd_conv_ch), SSD_CONV ** -0.5),
        "ssd_conv_b": nrm((DEPTH, ssd_conv_ch), 0.02),
        "ssd_dt_bias": dt_bias(SSD_HEADS),
        "ssd_a_log": jnp.log(unif((DEPTH, SSD_HEADS), 1.0, 16.0)),
        "ssd_d": 1.0 + nrm((DEPTH, SSD_HEADS), 0.02),
        "ssd_norm_w": 1.0 + nrm((DEPTH, SSD_INNER), 0.02),
        "gdn_conv_w": nrm((DEPTH, GDN_CONV, 3 * GDN_WIDTH), GDN_CONV ** -0.5),
        "gdn_a_log": jnp.log(unif((DEPTH, GDN_HEADS), 1.0, 16.0)),
        "gdn_dt_bias": dt_bias(GDN_HEADS),
        "gdn_norm_w": 1.0 + nrm((DEPTH, GDN_HEAD_DIM), 0.02),
        "gla_gate_w2": nrm((DEPTH, GLA_GATE_RANK, GLA_K), GLA_GATE_RANK ** -0.5),
        "gla_gate_b": nrm((DEPTH, GLA_K), 0.1),
        "gla_norm_w": 1.0 + nrm((DEPTH, GLA_VAL_DIM), 0.02),
        "w_br_ssd": nrm((DEPTH, SSD_INNER, Dm), BETA * SSD_INNER ** -0.5),
        "w_br_gdn": nrm((DEPTH, GDN_WIDTH, Dm), BETA * GDN_WIDTH ** -0.5),
        "w_br_gla": nrm((DEPTH, GLA_V, Dm), BETA * GLA_V ** -0.5),
        "w_out": nrm((DEPTH, Dm, Dm), BETA * Dm ** -0.5),
        "ln1_g": 1.0 + nrm((DEPTH, Dm), 0.02),
        "ln1_b": nrm((DEPTH, Dm), 0.02),
        "ffn_w_up": nrm((DEPTH, Dm, 2 * FFN_DIM), Dm ** -0.5),
        "ffn_conv_w": nrm((DEPTH, FFN_CONV, 2 * FFN_DIM), FFN_CONV ** -0.5),
        "ffn_conv_b": nrm((DEPTH, 2 * FFN_DIM), 0.02),
        "ffn_w_down": nrm((DEPTH, FFN_DIM, Dm), BETA * FFN_DIM ** -0.5),
        "ln2_g": 1.0 + nrm((DEPTH, Dm), 0.02),
        "ln2_b": nrm((DEPTH, Dm), 0.02),
    }


def reference(x, w_in, ssd_conv_w, ssd_conv_b, ssd_dt_bias, ssd_a_log, ssd_d, ssd_norm_w,
              gdn_conv_w, gdn_a_log, gdn_dt_bias, gdn_norm_w, gla_gate_w2, gla_gate_b, gla_norm_w,
              w_br_ssd, w_br_gdn, w_br_gla, w_out, ln1_g, ln1_b,
              ffn_w_up, ffn_conv_w, ffn_conv_b, ffn_w_down, ln2_g, ln2_b):
    Bsz, T, Dm = x.shape
    for l in range(DEPTH):
        h = x @ w_in[l]
        (z, xbc, dt_raw, gdn_qkv, gdn_a, gdn_b, gdn_g,
         gla_qkv, gla_glr, gla_r, gate_logits) = jnp.split(h, SPLIT_IDX, axis=-1)
        y_ssd = ssd_mixer(z, xbc, dt_raw, ssd_conv_w[l], ssd_conv_b[l], ssd_dt_bias[l],
                          ssd_a_log[l], ssd_d[l], ssd_norm_w[l]) @ w_br_ssd[l]
        y_gdn = gdn_mixer(gdn_qkv, gdn_a, gdn_b, gdn_g, gdn_conv_w[l], gdn_a_log[l],
                          gdn_dt_bias[l], gdn_norm_w[l]) @ w_br_gdn[l]
        y_gla = gla_mixer(gla_qkv, gla_glr, gla_r, gla_gate_w2[l], gla_gate_b[l],
                          gla_norm_w[l]) @ w_br_gla[l]
        gates = jax.nn.sigmoid(gate_logits).reshape(Bsz, T, N_BRANCH, Dm)
        mix = gates[:, :, 0] * y_ssd + gates[:, :, 1] * y_gdn + gates[:, :, 2] * y_gla
        x = layer_norm(ALPHA * x + mix @ w_out[l], ln1_g[l], ln1_b[l])
        x = layer_norm(ALPHA * x + conv_glu_ffn(x, ffn_w_up[l], ffn_conv_w[l], ffn_conv_b[l],
                                                ffn_w_down[l]), ln2_g[l], ln2_b[l])
    return x
```

```python
import functools

import jax
import jax.numpy as jnp
from jax import lax
from jax.experimental import pallas as pl
from jax.experimental.pallas import tpu as pltpu

_F32 = jnp.float32
_BF16 = jnp.bfloat16

D_MODEL = 1024
DEPTH = 2
SSD_HEADS = 16
SSD_HEAD_DIM = 64
SSD_INNER = SSD_HEADS * SSD_HEAD_DIM
SSD_GROUPS = 2
SSD_STATE = 128
SSD_CONV = 4
GDN_HEADS = 4
GDN_HEAD_DIM = 128
GDN_WIDTH = GDN_HEADS * GDN_HEAD_DIM
GDN_CONV = 4
GDN_CHUNK = 64
GLA_HEADS = 4
GLA_KEY_DIM = 64
GLA_VAL_DIM = 128
GLA_K = GLA_HEADS * GLA_KEY_DIM
GLA_V = GLA_HEADS * GLA_VAL_DIM
GLA_GATE_RANK = 16
GLA_GATE_NORMALIZER = 16.0
GLA_CHUNK = 16
FFN_DIM = 2816
FFN_CONV = 3
N_BRANCH = 3
ALPHA = (2 * DEPTH) ** 0.25
LN_EPS = 1e-5
RMS_EPS = 1e-6

LANES = 128
SUBLANES = 8
TIME_BLOCK = 256
SSD_CHUNK = 128
ROW_BLOCK = 256
VMEM_LIMIT = 56 * 1024 * 1024

_OFF_Z = 0
_OFF_XBC = _OFF_Z + SSD_INNER
_OFF_DT = _OFF_XBC + SSD_INNER + 2 * SSD_GROUPS * SSD_STATE
_OFF_GDN_QKV = _OFF_DT + SSD_HEADS
_OFF_GDN_A = _OFF_GDN_QKV + 3 * GDN_WIDTH
_OFF_GDN_B = _OFF_GDN_A + GDN_HEADS
_OFF_GDN_G = _OFF_GDN_B + GDN_HEADS
_OFF_GLA_QKV = _OFF_GDN_G + GDN_WIDTH
_OFF_GLA_GLR = _OFF_GLA_QKV + 2 * GLA_K + GLA_V
_OFF_GLA_R = _OFF_GLA_GLR + GLA_GATE_RANK
_OFF_GATES = _OFF_GLA_R + GLA_V
_IN_DIM = _OFF_GATES + N_BRANCH * D_MODEL


def _log2(n):
    assert n & (n - 1) == 0
    return n.bit_length() - 1


def _mm(a, b):
    return jnp.dot(a.astype(_BF16), b.astype(_BF16), preferred_element_type=_F32)


def _mm_nt(a, b):
    return lax.dot_general(a.astype(_BF16), b.astype(_BF16), (((1,), (1,)), ((), ())),
                           preferred_element_type=_F32)


def _silu(v):
    return v * jax.nn.sigmoid(v)


def _softplus(v):
    return jnp.maximum(v, 0.0) + jnp.log1p(jnp.exp(-jnp.abs(v)))


def _split_bf16(v, n):
    parts, rest = [], v
    for i in range(n):
        p = rest.astype(_BF16)
        parts.append(p)
        if i + 1 < n:
            rest = rest - p.astype(_F32)
    return parts


def _chunk_masks(n, chunk):
    shift = _log2(chunk)
    r = lax.broadcasted_iota(jnp.int32, (n, n), 0)
    c = lax.broadcasted_iota(jnp.int32, (n, n), 1)
    same = (r >> shift) == (c >> shift)
    return same, jnp.logical_and(same, c <= r)


def _chunk_cumsum(v, chunk):
    n, width = v.shape
    same, tri = _chunk_masks(n, chunk)
    lhs = jnp.concatenate([jnp.where(tri, 1.0, 0.0), jnp.where(same, 1.0, 0.0)], axis=0).astype(_BF16)
    rhs = jnp.concatenate(_split_bf16(v, 3), axis=1)
    r = jnp.dot(lhs, rhs, preferred_element_type=_F32)
    s = r[:, :width] + r[:, width:2 * width] + r[:, 2 * width:]
    return s[:n], s[n:]


def _causal_dwconv(pre, tail_ref, w_ref, width):
    n = pre.shape[0]
    ext = jnp.concatenate([tail_ref[...], pre], axis=0)
    acc = pre * w_ref[width - 1:width, :]
    for k in range(width - 1):
        back = width - 1 - k
        acc = acc + ext[SUBLANES - back:SUBLANES - back + n, :] * w_ref[k:k + 1, :]
    tail_ref[...] = pre[n - SUBLANES:, :]
    return acc


def _layer_norm(v, g, b):
    mu = jnp.mean(v, axis=-1, keepdims=True)
    c = v - mu
    var = jnp.mean(c * c, axis=-1, keepdims=True)
    return c * lax.rsqrt(var + LN_EPS) * g + b


def _rms(v):
    return v * lax.rsqrt(jnp.mean(v * v, axis=-1, keepdims=True) + RMS_EPS)


def _l2norm(v):
    return v * lax.rsqrt(jnp.sum(v * v, axis=-1, keepdims=True) + RMS_EPS)


def _unit_lower_inverse(a):
    n = a.shape[0]
    r = lax.broadcasted_iota(jnp.int32, (n, n), 0)
    c = lax.broadcasted_iota(jnp.int32, (n, n), 1)
    p = -a
    t = jnp.where(r == c, 1.0, 0.0) + p
    for _ in range(n.bit_length() - 2):
        p = _mm(p, p)
        t = t + _mm(t, p)
    return t


def _ssd_kernel(x_ref, wz_ref, wxbc_ref, wdt_ref, cw_ref, cb_ref, dtb_ref, alog_ref, dskip_ref,
                nw_ref, wbr_ref, o_ref, tail_ref, st_ref, y_ref):
    tb = x_ref.shape[0]
    lc = SSD_CHUNK
    gw = SSD_INNER // SSD_GROUPS

    @pl.when(pl.program_id(1) == 0)
    def _():
        tail_ref[...] = jnp.zeros_like(tail_ref)
        st_ref[...] = jnp.zeros_like(st_ref)

    xb = x_ref[...].astype(_BF16)
    z = jnp.dot(xb, wz_ref[...], preferred_element_type=_F32)
    pre = jnp.dot(xb, wxbc_ref[...], preferred_element_type=_F32)
    dtr = jnp.dot(xb, wdt_ref[...], preferred_element_type=_F32)

    xbc = _silu(_causal_dwconv(pre, tail_ref, cw_ref, SSD_CONV) + cb_ref[...])
    xs = xbc[:, :SSD_INNER]
    bm = xbc[:, SSD_INNER:SSD_INNER + SSD_GROUPS * SSD_STATE]
    cm = xbc[:, SSD_INNER + SSD_GROUPS * SSD_STATE:]

    dt = _softplus(dtr + dtb_ref[...])
    a = dt * (-jnp.exp(alog_ref[...]))
    a_cs, a_end = _chunk_cumsum(a, lc)
    a_cs_t = a_cs.T
    dt_t = dt.T

    hr = lax.broadcasted_iota(jnp.int32, (LANES, SSD_INNER), 0)
    hc = lax.broadcasted_iota(jnp.int32, (LANES, SSD_INNER), 1)
    expand = jnp.where(hr == (hc >> _log2(SSD_HEAD_DIM)), 1.0, 0.0).astype(_BF16)
    factors = jnp.concatenate([jnp.exp(a_cs), jnp.exp(a_end - a_cs) * dt], axis=0)
    fx = sum(jnp.dot(p, expand, preferred_element_type=_F32) for p in _split_bf16(factors, 2))
    dec_in_x, w_st_x = fx[:tb], fx[tb:]

    lr = lax.broadcasted_iota(jnp.int32, (lc, lc), 0)
    lcol = lax.broadcasted_iota(jnp.int32, (lc, lc), 1)
    causal = lcol <= lr
    lane = lax.broadcasted_iota(jnp.int32, (lc, LANES), 1)
    half_mask = (lane < SSD_HEAD_DIM, lane >= SSD_HEAD_DIM)

    for c in range(tb // lc):
        r0, r1 = c * lc, (c + 1) * lc
        for g in range(SSD_GROUPS):
            cg = cm[r0:r1, g * SSD_STATE:(g + 1) * SSD_STATE]
            bg = bm[r0:r1, g * SSD_STATE:(g + 1) * SSD_STATE]
            cb = _mm_nt(cg, bg)
            st = st_ref[g]
            y_inter = _mm(cg, st) * dec_in_x[r0:r1, g * gw:(g + 1) * gw]
            for blk in range(gw // LANES):
                col0 = g * gw + blk * LANES
                xs_blk = xs[r0:r1, col0:col0 + LANES]
                acc = y_inter[:, blk * LANES:(blk + 1) * LANES]
                for half in range(2):
                    h = col0 // SSD_HEAD_DIM + half
                    seg = a_cs[r0:r1, h:h + 1] - a_cs_t[h:h + 1, r0:r1]
                    dec = jnp.where(causal, jnp.exp(jnp.minimum(seg, 0.0)), 0.0)
                    wm = cb * dec * dt_t[h:h + 1, r0:r1]
                    acc = acc + _mm(wm, jnp.where(half_mask[half], xs_blk, 0.0))
                y_ref[r0:r1, col0:col0 + LANES] = acc
            xw = xs[r0:r1, g * gw:(g + 1) * gw] * w_st_x[r0:r1, g * gw:(g + 1) * gw]
            st_ref[g] = st * dec_in_x[r1 - 1:r1, g * gw:(g + 1) * gw] + _mm(bg.T, xw)

    y = (y_ref[...] + dskip_ref[...] * xs) * _silu(z)
    y = jnp.concatenate([_rms(y[:, g * gw:(g + 1) * gw]) for g in range(SSD_GROUPS)], axis=1)
    o_ref[...] = _mm(y * nw_ref[...], wbr_ref[...])


def _gdn_kernel(x_ref, wqkv_ref, wab_ref, wg_ref, cw_ref, alog_ref, dtb_ref, nw_ref, wbr_ref,
                o_ref, tail_ref, s_ref, oacc_ref):
    tb = x_ref.shape[0]
    lc = GDN_CHUNK
    dh = GDN_HEAD_DIM

    @pl.when(pl.program_id(1) == 0)
    def _():
        tail_ref[...] = jnp.zeros_like(tail_ref)
        s_ref[...] = jnp.zeros_like(s_ref)

    xb = x_ref[...].astype(_BF16)
    pre = jnp.dot(xb, wqkv_ref[...], preferred_element_type=_F32)
    ab = jnp.dot(xb, wab_ref[...], preferred_element_type=_F32)
    gate = jnp.dot(xb, wg_ref[...], preferred_element_type=_F32)
    qkv = _silu(_causal_dwconv(pre, tail_ref, cw_ref, GDN_CONV))

    beta = jax.nn.sigmoid(ab[:, LANES:])
    gdec = -jnp.exp(alog_ref[...]) * _softplus(ab[:, :LANES] + dtb_ref[...])
    g_cs, g_end = _chunk_cumsum(gdec, lc)
    g_cs_t = g_cs.T
    e_cs = jnp.exp(g_cs)
    e_rem = jnp.exp(g_end - g_cs)
    e_end = jnp.exp(g_end)

    lr = lax.broadcasted_iota(jnp.int32, (lc, lc), 0)
    lcol = lax.broadcasted_iota(jnp.int32, (lc, lc), 1)
    incl = lcol <= lr
    strict = lcol < lr

    for h in range(GDN_HEADS):
        q = _l2norm(qkv[:, h * dh:(h + 1) * dh]) * (dh ** -0.5)
        k = _l2norm(qkv[:, GDN_WIDTH + h * dh:GDN_WIDTH + (h + 1) * dh])
        v = qkv[:, 2 * GDN_WIDTH + h * dh:2 * GDN_WIDTH + (h + 1) * dh]
        bcol = beta[:, h:h + 1]
        kb = k * bcol
        vb = v * bcol
        kbe = kb * e_cs[:, h:h + 1]
        q_dec = q * e_cs[:, h:h + 1]
        k_dec = k * e_rem[:, h:h + 1]
        s = s_ref[h]
        for c in range(tb // lc):
            r0, r1 = c * lc, (c + 1) * lc
            seg = g_cs[r0:r1, h:h + 1] - g_cs_t[h:h + 1, r0:r1]
            gamma = jnp.where(incl, jnp.exp(jnp.minimum(seg, 0.0)), 0.0)
            a_mat = jnp.where(strict, _mm_nt(kb[r0:r1], k[r0:r1]) * gamma, 0.0)
            t_mat = _unit_lower_inverse(a_mat)
            uw = _mm(t_mat, jnp.concatenate([vb[r0:r1], kbe[r0:r1]], axis=1))
            attn = jnp.where(incl, _mm_nt(q[r0:r1], k[r0:r1]) * gamma, 0.0)
            ws = _mm(jnp.concatenate([uw[:, dh:], q_dec[r0:r1]], axis=0), s)
            v_new = uw[:, :dh] - ws[:lc]
            oacc_ref[r0:r1, h * dh:(h + 1) * dh] = ws[lc:] + _mm(attn, v_new)
            s = s * e_end[r1 - 1:r1, h:h + 1] + _mm(k_dec[r0:r1].T, v_new)
        s_ref[h] = s

    o = oacc_ref[...]
    o = jnp.concatenate([_rms(o[:, h * dh:(h + 1) * dh]) * nw_ref[...] for h in range(GDN_HEADS)],
                        axis=1)
    o_ref[...] = _mm(o * _silu(gate), wbr_ref[...])


def _gla_kernel(x_ref, wqkv_ref, wglr_ref, wr_ref, w2_ref, gb_ref, nw_ref, wbr_ref,
                o_ref, s_ref, oacc_ref):
    tb = x_ref.shape[0]
    lc = GLA_CHUNK
    dk, dv = GLA_KEY_DIM, GLA_VAL_DIM

    @pl.when(pl.program_id(1) == 0)
    def _():
        s_ref[...] = jnp.zeros_like(s_ref)

    xb = x_ref[...].astype(_BF16)
    qkv = jnp.dot(xb, wqkv_ref[...], preferred_element_type=_F32)
    glr = jnp.dot(xb, wglr_ref[...], preferred_element_type=_F32)
    rgate = jnp.dot(xb, wr_ref[...], preferred_element_type=_F32)
    q = qkv[:, :GLA_K] * (dk ** -0.5)
    k = qkv[:, GLA_K:2 * GLA_K]
    v = qkv[:, 2 * GLA_K:]

    gk = -_softplus(-(_mm(glr, w2_ref[...]) + gb_ref[...])) / GLA_GATE_NORMALIZER
    b_cs, b_end = _chunk_cumsum(gk, lc)
    q_e = q * jnp.exp(b_cs)
    k_e = k * jnp.exp(-b_cs)
    k_d_t = (k * jnp.exp(b_end - b_cs)).T
    d_t = jnp.exp(b_end).T

    _, tri = _chunk_masks(tb, lc)
    lane_k = lax.broadcasted_iota(jnp.int32, (tb, GLA_K), 1)
    q_heads = [jnp.where((lane_k >> _log2(dk)) == h, q_e, 0.0) for h in range(GLA_HEADS)]
    tcol = lax.broadcasted_iota(jnp.int32, (GLA_K, tb), 1)

    for h in range(GLA_HEADS):
        a_mat = jnp.where(tri, _mm_nt(q_heads[h], k_e), 0.0)
        oacc_ref[:, h * dv:(h + 1) * dv] = _mm(a_mat, v[:, h * dv:(h + 1) * dv])

    vb = v.astype(_BF16)
    s = [s_ref[h] for h in range(GLA_HEADS)]
    for c in range(tb // lc):
        r0, r1 = c * lc, (c + 1) * lc
        lhs = jnp.concatenate([qh[r0:r1] for qh in q_heads], axis=0)
        res = _mm(lhs, jnp.concatenate(s, axis=0))
        k_c = jnp.where((tcol >> _log2(lc)) == c, k_d_t, 0.0).astype(_BF16)
        for h in range(GLA_HEADS):
            oacc_ref[r0:r1, h * dv:(h + 1) * dv] += res[h * lc:(h + 1) * lc]
            upd = jnp.dot(k_c[h * dk:(h + 1) * dk], vb[:, h * dv:(h + 1) * dv],
                          preferred_element_type=_F32)
            s[h] = s[h] * d_t[h * dk:(h + 1) * dk, r0:r0 + 1] + upd
    for h in range(GLA_HEADS):
        s_ref[h] = s[h]

    o = oacc_ref[...]
    o = jnp.concatenate([_rms(o[:, h * dv:(h + 1) * dv]) * nw_ref[...] for h in range(GLA_HEADS)],
                        axis=1)
    o_ref[...] = _mm(o * _silu(rgate), wbr_ref[...])


def _merge_kernel(x_ref, y0_ref, y1_ref, y2_ref, wg_ref, wo_ref, g_ref, b_ref, o_ref):
    x = x_ref[...]
    logits = jnp.dot(x.astype(_BF16), wg_ref[...], preferred_element_type=_F32)
    mix = (jax.nn.sigmoid(logits[:, :D_MODEL]) * y0_ref[...]
           + jax.nn.sigmoid(logits[:, D_MODEL:2 * D_MODEL]) * y1_ref[...]
           + jax.nn.sigmoid(logits[:, 2 * D_MODEL:]) * y2_ref[...])
    o_ref[...] = _layer_norm(ALPHA * x + _mm(mix, wo_ref[...]), g_ref[...], b_ref[...])


def _ffn_kernel(x_ref, wup_ref, cw_ref, cb_ref, wdn_ref, g_ref, b_ref, o_ref, tail_ref):
    @pl.when(pl.program_id(1) == 0)
    def _():
        tail_ref[...] = jnp.zeros_like(tail_ref)

    x = x_ref[...]
    pre = jnp.dot(x.astype(_BF16), wup_ref[...], preferred_element_type=_F32)
    hid = _causal_dwconv(pre, tail_ref, cw_ref, FFN_CONV) + cb_ref[...]
    act = _silu(hid[:, :FFN_DIM]) * hid[:, FFN_DIM:]
    o_ref[...] = _layer_norm(ALPHA * x + _mm(act, wdn_ref[...]), g_ref[...], b_ref[...])


def _resident(a):
    return pl.BlockSpec(a.shape, lambda *_: (0,) * a.ndim)


def _sequence_call(body, name, x, params, out_dim, scratch):
    bsz, seq, dm = x.shape
    tb = TIME_BLOCK
    assert seq % tb == 0 and dm == D_MODEL
    return pl.pallas_call(
        body,
        grid=(bsz, seq // tb),
        in_specs=[pl.BlockSpec((None, tb, dm), lambda b, j: (b, j, 0))] + [_resident(p) for p in params],
        out_specs=pl.BlockSpec((None, tb, out_dim), lambda b, j: (b, j, 0)),
        out_shape=jax.ShapeDtypeStruct((bsz, seq, out_dim), _F32),
        scratch_shapes=scratch,
        compiler_params=pltpu.CompilerParams(dimension_semantics=("parallel", "arbitrary"),
                                             vmem_limit_bytes=VMEM_LIMIT),
        name=name,
    )(x, *params)


def _merge_call(x, ys, params):
    bsz, seq, dm = x.shape
    rows = bsz * seq
    rb = ROW_BLOCK
    assert rows % rb == 0
    row_spec = pl.BlockSpec((rb, dm), lambda i: (i, 0))
    out = pl.pallas_call(
        _merge_kernel,
        grid=(rows // rb,),
        in_specs=[row_spec] * 4 + [_resident(p) for p in params],
        out_specs=row_spec,
        out_shape=jax.ShapeDtypeStruct((rows, dm), _F32),
        compiler_params=pltpu.CompilerParams(dimension_semantics=("parallel",),
                                             vmem_limit_bytes=VMEM_LIMIT),
        name="merge",
    )(x.reshape(rows, dm), *[y.reshape(rows, dm) for y in ys], *params)
    return out.reshape(bsz, seq, dm)


def _pad_cols(a, width):
    return jnp.pad(a, ((0, 0), (0, width - a.shape[1])))


def _row(v, width=None):
    v = v.reshape(1, -1).astype(_F32)
    return v if width is None else _pad_cols(v, width)


def kernel(x, w_in, ssd_conv_w, ssd_conv_b, ssd_dt_bias, ssd_a_log, ssd_d, ssd_norm_w, gdn_conv_w, gdn_a_log, gdn_dt_bias, gdn_norm_w, gla_gate_w2, gla_gate_b, gla_norm_w, w_br_ssd, w_br_gdn, w_br_gla, w_out, ln1_g, ln1_b, ffn_w_up, ffn_conv_w, ffn_conv_b, ffn_w_down, ln2_g, ln2_b):
    assert w_in.shape == (DEPTH, D_MODEL, _IN_DIM)
    bf = lambda a: a.astype(_BF16)
    xbc_w = SSD_INNER + 2 * SSD_GROUPS * SSD_STATE
    for l in range(DEPTH):
        w = w_in[l]
        ssd_params = (
            bf(w[:, _OFF_Z:_OFF_XBC]),
            bf(w[:, _OFF_XBC:_OFF_DT]),
            bf(_pad_cols(w[:, _OFF_DT:_OFF_GDN_QKV], LANES)),
            ssd_conv_w[l], _row(ssd_conv_b[l]),
            _row(ssd_dt_bias[l], LANES), _row(ssd_a_log[l], LANES),
            _row(jnp.repeat(ssd_d[l], SSD_HEAD_DIM)), _row(ssd_norm_w[l]),
            bf(w_br_ssd[l]),
        )
        y_ssd = _sequence_call(
            _ssd_kernel, "ssd", x, ssd_params, D_MODEL,
            [pltpu.VMEM((SUBLANES, xbc_w), _F32),
             pltpu.VMEM((SSD_GROUPS, SSD_STATE, SSD_INNER // SSD_GROUPS), _F32),
             pltpu.VMEM((TIME_BLOCK, SSD_INNER), _F32)])
        w_ab = jnp.concatenate([_pad_cols(w[:, _OFF_GDN_A:_OFF_GDN_B], LANES),
                                _pad_cols(w[:, _OFF_GDN_B:_OFF_GDN_G], LANES)], axis=1)
        gdn_params = (
            bf(w[:, _OFF_GDN_QKV:_OFF_GDN_A]), bf(w_ab), bf(w[:, _OFF_GDN_G:_OFF_GLA_QKV]),
            gdn_conv_w[l], _row(gdn_a_log[l], LANES), _row(gdn_dt_bias[l], LANES),
            _row(gdn_norm_w[l]), bf(w_br_gdn[l]),
        )
        y_gdn = _sequence_call(
            _gdn_kernel, "gdn", x, gdn_params, D_MODEL,
            [pltpu.VMEM((SUBLANES, 3 * GDN_WIDTH), _F32),
             pltpu.VMEM((GDN_HEADS, GDN_HEAD_DIM, GDN_HEAD_DIM), _F32),
             pltpu.VMEM((TIME_BLOCK, GDN_WIDTH), _F32)])
        w2 = jnp.pad(gla_gate_w2[l], ((0, LANES - GLA_GATE_RANK), (0, 0)))
        gla_params = (
            bf(w[:, _OFF_GLA_QKV:_OFF_GLA_GLR]),
            bf(_pad_cols(w[:, _OFF_GLA_GLR:_OFF_GLA_R], LANES)),
            bf(w[:, _OFF_GLA_R:_OFF_GATES]),
            bf(w2), _row(gla_gate_b[l]), _row(gla_norm_w[l]), bf(w_br_gla[l]),
        )
        y_gla = _sequence_call(
            _gla_kernel, "gla", x, gla_params, D_MODEL,
            [pltpu.VMEM((GLA_HEADS, GLA_KEY_DIM, GLA_VAL_DIM), _F32),
             pltpu.VMEM((TIME_BLOCK, GLA_V), _F32)])
        x = _merge_call(x, (y_ssd, y_gdn, y_gla),
                        (bf(w[:, _OFF_GATES:]), bf(w_out[l]), _row(ln1_g[l]), _row(ln1_b[l])))
        ffn_params = (bf(ffn_w_up[l]), ffn_conv_w[l], _row(ffn_conv_b[l]), bf(ffn_w_down[l]),
                      _row(ln2_g[l]), _row(ln2_b[l]))
        x = _sequence_call(_ffn_kernel, "ffn", x, ffn_params, D_MODEL,
                           [pltpu.VMEM((SUBLANES, 2 * FFN_DIM), _F32)])
    return x
```

```python
import functools

import jax
import jax.numpy as jnp
from jax import lax
from jax.experimental import pallas as pl
from jax.experimental.pallas import tpu as pltpu

_F32 = jnp.float32
_BF16 = jnp.bfloat16

D_MODEL = 1024
DEPTH = 2
SSD_HEADS = 16
SSD_HEAD_DIM = 64
SSD_INNER = SSD_HEADS * SSD_HEAD_DIM
SSD_GROUPS = 2
SSD_STATE = 128
SSD_CONV = 4
GDN_HEADS = 4
GDN_HEAD_DIM = 128
GDN_WIDTH = GDN_HEADS * GDN_HEAD_DIM
GDN_CONV = 4
GDN_CHUNK = 64
GLA_HEADS = 4
GLA_KEY_DIM = 64
GLA_VAL_DIM = 128
GLA_K = GLA_HEADS * GLA_KEY_DIM
GLA_V = GLA_HEADS * GLA_VAL_DIM
GLA_GATE_RANK = 16
GLA_GATE_NORMALIZER = 16.0
GLA_CHUNK = 16
FFN_DIM = 2816
FFN_CONV = 3
N_BRANCH = 3
ALPHA = (2 * DEPTH) ** 0.25
LN_EPS = 1e-5
RMS_EPS = 1e-6

LANES = 128
SUBLANES = 8
TIME_BLOCK = 256
SSD_CHUNK = 128
ROW_BLOCK = 256
VMEM_LIMIT = 56 * 1024 * 1024

_OFF_Z = 0
_OFF_XBC = _OFF_Z + SSD_INNER
_OFF_DT = _OFF_XBC + SSD_INNER + 2 * SSD_GROUPS * SSD_STATE
_OFF_GDN_QKV = _OFF_DT + SSD_HEADS
_OFF_GDN_A = _OFF_GDN_QKV + 3 * GDN_WIDTH
_OFF_GDN_B = _OFF_GDN_A + GDN_HEADS
_OFF_GDN_G = _OFF_GDN_B + GDN_HEADS
_OFF_GLA_QKV = _OFF_GDN_G + GDN_WIDTH
_OFF_GLA_GLR = _OFF_GLA_QKV + 2 * GLA_K + GLA_V
_OFF_GLA_R = _OFF_GLA_GLR + GLA_GATE_RANK
_OFF_GATES = _OFF_GLA_R + GLA_V
_IN_DIM = _OFF_GATES + N_BRANCH * D_MODEL


def _log2(n):
    assert n & (n - 1) == 0
    return n.bit_length() - 1


def _mm(a, b):
    return jnp.dot(a.astype(_BF16), b.astype(_BF16), preferred_element_type=_F32)


def _mm_nt(a, b):
    return lax.dot_general(a.astype(_BF16), b.astype(_BF16), (((1,), (1,)), ((), ())),
                           preferred_element_type=_F32)


def _bmm(a, b):
    return lax.dot_general(a.astype(_BF16), b.astype(_BF16), (((2,), (1,)), ((0,), (0,))),
                           preferred_element_type=_F32)


def _bmm_nt(a, b):
    return lax.dot_general(a.astype(_BF16), b.astype(_BF16), (((2,), (2,)), ((0,), (0,))),
                           preferred_element_type=_F32)


def _silu(v):
    return v * jax.nn.sigmoid(v)


def _softplus(v):
    return jnp.maximum(v, 0.0) + jnp.log1p(jnp.exp(-jnp.abs(v)))


def _split_bf16(v, n):
    parts, rest = [], v
    for i in range(n):
        p = rest.astype(_BF16)
        parts.append(p)
        if i + 1 < n:
            rest = rest - p.astype(_F32)
    return parts


def _chunk_masks(n, chunk):
    shift = _log2(chunk)
    r = lax.broadcasted_iota(jnp.int32, (n, n), 0)
    c = lax.broadcasted_iota(jnp.int32, (n, n), 1)
    same = (r >> shift) == (c >> shift)
    return same, jnp.logical_and(same, c <= r)


def _chunk_cumsum(v, chunk):
    n, width = v.shape
    same, tri = _chunk_masks(n, chunk)
    lhs = jnp.concatenate([jnp.where(tri, 1.0, 0.0), jnp.where(same, 1.0, 0.0)], axis=0).astype(_BF16)
    rhs = jnp.concatenate(_split_bf16(v, 3), axis=1)
    r = jnp.dot(lhs, rhs, preferred_element_type=_F32)
    s = r[:, :width] + r[:, width:2 * width] + r[:, 2 * width:]
    return s[:n], s[n:]


def _causal_dwconv(pre, tail_ref, w_ref, width):
    n = pre.shape[0]
    ext = jnp.concatenate([tail_ref[...], pre], axis=0)
    acc = pre * w_ref[width - 1:width, :]
    for k in range(width - 1):
        back = width - 1 - k
        acc = acc + ext[SUBLANES - back:SUBLANES - back + n, :] * w_ref[k:k + 1, :]
    tail_ref[...] = pre[n - SUBLANES:, :]
    return acc


def _layer_norm(v, g, b):
    mu = jnp.mean(v, axis=-1, keepdims=True)
    c = v - mu
    var = jnp.mean(c * c, axis=-1, keepdims=True)
    return c * lax.rsqrt(var + LN_EPS) * g + b


def _rms(v):
    return v * lax.rsqrt(jnp.mean(v * v, axis=-1, keepdims=True) + RMS_EPS)


def _l2norm(v):
    return v * lax.rsqrt(jnp.sum(v * v, axis=-1, keepdims=True) + RMS_EPS)


def _unit_lower_inverse(a):
    n = a.shape[-1]
    r = lax.broadcasted_iota(jnp.int32, (n, n), 0)
    c = lax.broadcasted_iota(jnp.int32, (n, n), 1)
    p = -a
    t = jnp.where(r == c, 1.0, 0.0) + p
    for _ in range(_log2(n) - 1):
        p = _bmm(p, p)
        t = t + _bmm(t, p)
    return t


def _ssd_kernel(x_ref, wz_ref, wxbc_ref, wdt_ref, cw_ref, cb_ref, dtb_ref, alog_ref, dskip_ref,
                nw_ref, wbr_ref, o_ref, tail_ref, st_ref, y_ref):
    tb = x_ref.shape[0]
    lc = SSD_CHUNK
    gw = SSD_INNER // SSD_GROUPS

    @pl.when(pl.program_id(1) == 0)
    def _():
        tail_ref[...] = jnp.zeros_like(tail_ref)
        st_ref[...] = jnp.zeros_like(st_ref)

    xb = x_ref[...].astype(_BF16)
    z = jnp.dot(xb, wz_ref[...], preferred_element_type=_F32)
    pre = jnp.dot(xb, wxbc_ref[...], preferred_element_type=_F32)
    dtr = jnp.dot(xb, wdt_ref[...], preferred_element_type=_F32)

    xbc = _silu(_causal_dwconv(pre, tail_ref, cw_ref, SSD_CONV) + cb_ref[...])
    xs = xbc[:, :SSD_INNER]
    bm = xbc[:, SSD_INNER:SSD_INNER + SSD_GROUPS * SSD_STATE]
    cm = xbc[:, SSD_INNER + SSD_GROUPS * SSD_STATE:]

    dt = _softplus(dtr + dtb_ref[...])
    a = dt * (-jnp.exp(alog_ref[...]))
    a_cs, a_end = _chunk_cumsum(a, lc)
    a_cs_t = a_cs.T
    dt_t = dt.T

    hr = lax.broadcasted_iota(jnp.int32, (LANES, SSD_INNER), 0)
    hc = lax.broadcasted_iota(jnp.int32, (LANES, SSD_INNER), 1)
    expand = jnp.where(hr == (hc >> _log2(SSD_HEAD_DIM)), 1.0, 0.0).astype(_BF16)
    factors = jnp.concatenate([jnp.exp(a_cs), jnp.exp(a_end - a_cs) * dt], axis=0)
    fx = sum(jnp.dot(p, expand, preferred_element_type=_F32) for p in _split_bf16(factors, 2))
    dec_in_x, w_st_x = fx[:tb], fx[tb:]

    lr = lax.broadcasted_iota(jnp.int32, (lc, lc), 0)
    lcol = lax.broadcasted_iota(jnp.int32, (lc, lc), 1)
    causal = lcol <= lr
    lane = lax.broadcasted_iota(jnp.int32, (lc, LANES), 1)
    half_mask = (lane < SSD_HEAD_DIM, lane >= SSD_HEAD_DIM)

    for c in range(tb // lc):
        r0, r1 = c * lc, (c + 1) * lc
        for g in range(SSD_GROUPS):
            cg = cm[r0:r1, g * SSD_STATE:(g + 1) * SSD_STATE]
            bg = bm[r0:r1, g * SSD_STATE:(g + 1) * SSD_STATE]
            cb = _mm_nt(cg, bg)
            st = st_ref[g]
            y_inter = _mm(cg, st) * dec_in_x[r0:r1, g * gw:(g + 1) * gw]
            for blk in range(gw // LANES):
                col0 = g * gw + blk * LANES
                xs_blk = xs[r0:r1, col0:col0 + LANES]
                acc = y_inter[:, blk * LANES:(blk + 1) * LANES]
                for half in range(2):
                    h = col0 // SSD_HEAD_DIM + half
                    seg = a_cs[r0:r1, h:h + 1] - a_cs_t[h:h + 1, r0:r1]
                    dec = jnp.where(causal, jnp.exp(jnp.minimum(seg, 0.0)), 0.0)
                    wm = cb * dec * dt_t[h:h + 1, r0:r1]
                    acc = acc + _mm(wm, jnp.where(half_mask[half], xs_blk, 0.0))
                y_ref[r0:r1, col0:col0 + LANES] = acc
            xw = xs[r0:r1, g * gw:(g + 1) * gw] * w_st_x[r0:r1, g * gw:(g + 1) * gw]
            st_ref[g] = st * dec_in_x[r1 - 1:r1, g * gw:(g + 1) * gw] + _mm(bg.T, xw)

    y = (y_ref[...] + dskip_ref[...] * xs) * _silu(z)
    y = jnp.concatenate([_rms(y[:, g * gw:(g + 1) * gw]) for g in range(SSD_GROUPS)], axis=1)
    o_ref[...] = _mm(y * nw_ref[...], wbr_ref[...])


def _gdn_kernel(x_ref, wqkv_ref, wab_ref, wg_ref, cw_ref, alog_ref, dtb_ref, nw_ref, wbr_ref,
                o_ref, tail_ref, s_ref, oacc_ref):
    tb = x_ref.shape[0]
    lc = GDN_CHUNK
    dh = GDN_HEAD_DIM

    @pl.when(pl.program_id(1) == 0)
    def _():
        tail_ref[...] = jnp.zeros_like(tail_ref)
        s_ref[...] = jnp.zeros_like(s_ref)

    xb = x_ref[...].astype(_BF16)
    pre = jnp.dot(xb, wqkv_ref[...], preferred_element_type=_F32)
    ab = jnp.dot(xb, wab_ref[...], preferred_element_type=_F32)
    gate = jnp.dot(xb, wg_ref[...], preferred_element_type=_F32)
    qkv = _silu(_causal_dwconv(pre, tail_ref, cw_ref, GDN_CONV))

    beta = jax.nn.sigmoid(ab[:, LANES:])
    gdec = -jnp.exp(alog_ref[...]) * _softplus(ab[:, :LANES] + dtb_ref[...])
    g_cs, g_end = _chunk_cumsum(gdec, lc)
    g_cs_t = g_cs.T
    e_cs = jnp.exp(g_cs)
    e_rem = jnp.exp(g_end - g_cs)
    e_end = jnp.exp(g_end)

    lr = lax.broadcasted_iota(jnp.int32, (lc, lc), 0)
    lcol = lax.broadcasted_iota(jnp.int32, (lc, lc), 1)
    incl = lcol <= lr
    strict = lcol < lr

    nc = tb // lc
    heads = []
    for h in range(GDN_HEADS):
        q = _l2norm(qkv[:, h * dh:(h + 1) * dh]) * (dh ** -0.5)
        k = _l2norm(qkv[:, GDN_WIDTH + h * dh:GDN_WIDTH + (h + 1) * dh])
        v = qkv[:, 2 * GDN_WIDTH + h * dh:2 * GDN_WIDTH + (h + 1) * dh]
        bcol = beta[:, h:h + 1]
        kb = k * bcol
        heads.append(dict(q=q, k=k, kb=kb, vb=v * bcol, kbe=kb * e_cs[:, h:h + 1],
                          q_dec=q * e_cs[:, h:h + 1], k_dec=k * e_rem[:, h:h + 1]))
    order = [(c, h) for c in range(nc) for h in range(GDN_HEADS)]

    def stack(name):
        return jnp.stack([heads[h][name][c * lc:(c + 1) * lc] for c, h in order], axis=0)

    def gamma_of(c, h):
        seg = g_cs[c * lc:(c + 1) * lc, h:h + 1] - g_cs_t[h:h + 1, c * lc:(c + 1) * lc]
        return jnp.where(incl, jnp.exp(jnp.minimum(seg, 0.0)), 0.0)

    k3 = stack("k")
    gamma3 = jnp.stack([gamma_of(c, h) for c, h in order], axis=0)
    a3 = jnp.where(strict, _bmm_nt(stack("kb"), k3) * gamma3, 0.0)
    attn3 = jnp.where(incl, _bmm_nt(stack("q"), k3) * gamma3, 0.0)
    uw3 = _bmm(_unit_lower_inverse(a3), jnp.concatenate([stack("vb"), stack("kbe")], axis=2))
    q_dec3 = stack("q_dec")
    k_dec_t3 = jnp.stack([heads[h]["k_dec"][c * lc:(c + 1) * lc].T for c, h in order], axis=0)

    s = s_ref[...]
    for c in range(nc):
        b0, b1 = c * GDN_HEADS, (c + 1) * GDN_HEADS
        ws = _bmm(jnp.concatenate([uw3[b0:b1, :, dh:], q_dec3[b0:b1]], axis=1), s)
        v_new = uw3[b0:b1, :, :dh] - ws[:, :lc]
        o_c = ws[:, lc:] + _bmm(attn3[b0:b1], v_new)
        decay = jnp.stack([e_end[(c + 1) * lc - 1:(c + 1) * lc, h:h + 1] for h in range(GDN_HEADS)],
                          axis=0)
        s = s * decay + _bmm(k_dec_t3[b0:b1], v_new)
        for h in range(GDN_HEADS):
            oacc_ref[c * lc:(c + 1) * lc, h * dh:(h + 1) * dh] = o_c[h]
    s_ref[...] = s

    o = oacc_ref[...]
    o = jnp.concatenate([_rms(o[:, h * dh:(h + 1) * dh]) * nw_ref[...] for h in range(GDN_HEADS)],
                        axis=1)
    o_ref[...] = _mm(o * _silu(gate), wbr_ref[...])


def _gla_kernel(x_ref, wqkv_ref, wglr_ref, wr_ref, w2_ref, gb_ref, nw_ref, wbr_ref,
                o_ref, s_ref):
    tb = x_ref.shape[0]
    lc = GLA_CHUNK
    dk, dv = GLA_KEY_DIM, GLA_VAL_DIM

    @pl.when(pl.program_id(1) == 0)
    def _():
        s_ref[...] = jnp.zeros_like(s_ref)

    xb = x_ref[...].astype(_BF16)
    qkv = jnp.dot(xb, wqkv_ref[...], preferred_element_type=_F32)
    glr = jnp.dot(xb, wglr_ref[...], preferred_element_type=_F32)
    rgate = jnp.dot(xb, wr_ref[...], preferred_element_type=_F32)
    q = qkv[:, :GLA_K] * (dk ** -0.5)
    k = qkv[:, GLA_K:2 * GLA_K]
    v = qkv[:, 2 * GLA_K:]

    gk = -_softplus(-(_mm(glr, w2_ref[...]) + gb_ref[...])) / GLA_GATE_NORMALIZER
    b_cs, b_end = _chunk_cumsum(gk, lc)
    q_e = q * jnp.exp(b_cs)
    k_e = k * jnp.exp(-b_cs)
    k_d_t = (k * jnp.exp(b_end - b_cs)).T
    d_t = jnp.exp(b_end).T

    nc = tb // lc
    _, tri = _chunk_masks(tb, lc)
    lane_k = lax.broadcasted_iota(jnp.int32, (tb, GLA_K), 1)
    q_heads = [jnp.where((lane_k >> _log2(dk)) == h, q_e, 0.0) for h in range(GLA_HEADS)]
    vb = v.astype(_BF16)

    intra = [_mm(jnp.where(tri, _mm_nt(q_heads[h], k_e), 0.0), vb[:, h * dv:(h + 1) * dv])
             for h in range(GLA_HEADS)]

    urow = lax.broadcasted_iota(jnp.int32, (nc * dk, tb), 0)
    ucol = lax.broadcasted_iota(jnp.int32, (nc * dk, tb), 1)
    own_chunk = (urow >> _log2(dk)) == (ucol >> _log2(lc))
    upd = [_mm(jnp.where(own_chunk, jnp.tile(k_d_t[h * dk:(h + 1) * dk], (nc, 1)), 0.0),
               vb[:, h * dv:(h + 1) * dv]) for h in range(GLA_HEADS)]

    s = [s_ref[h] for h in range(GLA_HEADS)]
    entering = []
    for c in range(nc):
        entering.append(jnp.concatenate(s, axis=0))
        s = [s[h] * d_t[h * dk:(h + 1) * dk, c * lc:c * lc + 1] + upd[h][c * dk:(c + 1) * dk]
             for h in range(GLA_HEADS)]
    for h in range(GLA_HEADS):
        s_ref[h] = s[h]
    lhs3 = jnp.stack([jnp.concatenate([qh[c * lc:(c + 1) * lc] for qh in q_heads], axis=0)
                      for c in range(nc)], axis=0)
    res3 = _bmm(lhs3, jnp.stack(entering, axis=0))

    o = []
    for h in range(GLA_HEADS):
        inter = jnp.concatenate([res3[c, h * lc:(h + 1) * lc] for c in range(nc)], axis=0)
        o.append(_rms(intra[h] + inter) * nw_ref[...])
    o = jnp.concatenate(o, axis=1)
    o_ref[...] = _mm(o * _silu(rgate), wbr_ref[...])


def _merge_kernel(x_ref, y0_ref, y1_ref, y2_ref, wg_ref, wo_ref, g_ref, b_ref, o_ref):
    x = x_ref[...]
    logits = jnp.dot(x.astype(_BF16), wg_ref[...], preferred_element_type=_F32)
    mix = (jax.nn.sigmoid(logits[:, :D_MODEL]) * y0_ref[...]
           + jax.nn.sigmoid(logits[:, D_MODEL:2 * D_MODEL]) * y1_ref[...]
           + jax.nn.sigmoid(logits[:, 2 * D_MODEL:]) * y2_ref[...])
    o_ref[...] = _layer_norm(ALPHA * x + _mm(mix, wo_ref[...]), g_ref[...], b_ref[...])


def _ffn_kernel(x_ref, wup_ref, cw_ref, cb_ref, wdn_ref, g_ref, b_ref, o_ref, tail_ref):
    @pl.when(pl.program_id(1) == 0)
    def _():
        tail_ref[...] = jnp.zeros_like(tail_ref)

    x = x_ref[...]
    pre = jnp.dot(x.astype(_BF16), wup_ref[...], preferred_element_type=_F32)
    hid = _causal_dwconv(pre, tail_ref, cw_ref, FFN_CONV) + cb_ref[...]
    act = _silu(hid[:, :FFN_DIM]) * hid[:, FFN_DIM:]
    o_ref[...] = _layer_norm(ALPHA * x + _mm(act, wdn_ref[...]), g_ref[...], b_ref[...])


def _resident(a):
    return pl.BlockSpec(a.shape, lambda *_: (0,) * a.ndim)


def _sequence_call(body, name, x, params, out_dim, scratch):
    bsz, seq, dm = x.shape
    tb = TIME_BLOCK
    assert seq % tb == 0 and dm == D_MODEL
    return pl.pallas_call(
        body,
        grid=(bsz, seq // tb),
        in_specs=[pl.BlockSpec((None, tb, dm), lambda b, j: (b, j, 0))] + [_resident(p) for p in params],
        out_specs=pl.BlockSpec((None, tb, out_dim), lambda b, j: (b, j, 0)),
        out_shape=jax.ShapeDtypeStruct((bsz, seq, out_dim), _F32),
        scratch_shapes=scratch,
        compiler_params=pltpu.CompilerParams(dimension_semantics=("parallel", "arbitrary"),
                                             vmem_limit_bytes=VMEM_LIMIT),
        name=name,
    )(x, *params)


def _merge_call(x, ys, params):
    bsz, seq, dm = x.shape
    rows = bsz * seq
    rb = ROW_BLOCK
    assert rows % rb == 0
    row_spec = pl.BlockSpec((rb, dm), lambda i: (i, 0))
    out = pl.pallas_call(
        _merge_kernel,
        grid=(rows // rb,),
        in_specs=[row_spec] * 4 + [_resident(p) for p in params],
        out_specs=row_spec,
        out_shape=jax.ShapeDtypeStruct((rows, dm), _F32),
        compiler_params=pltpu.CompilerParams(dimension_semantics=("parallel",),
                                             vmem_limit_bytes=VMEM_LIMIT),
        name="merge",
    )(x.reshape(rows, dm), *[y.reshape(rows, dm) for y in ys], *params)
    return out.reshape(bsz, seq, dm)


def _pad_cols(a, width):
    return jnp.pad(a, ((0, 0), (0, width - a.shape[1])))


def _row(v, width=None):
    v = v.reshape(1, -1).astype(_F32)
    return v if width is None else _pad_cols(v, width)


def kernel(x, w_in, ssd_conv_w, ssd_conv_b, ssd_dt_bias, ssd_a_log, ssd_d, ssd_norm_w, gdn_conv_w, gdn_a_log, gdn_dt_bias, gdn_norm_w, gla_gate_w2, gla_gate_b, gla_norm_w, w_br_ssd, w_br_gdn, w_br_gla, w_out, ln1_g, ln1_b, ffn_w_up, ffn_conv_w, ffn_conv_b, ffn_w_down, ln2_g, ln2_b):
    assert w_in.shape == (DEPTH, D_MODEL, _IN_DIM)
    bf = lambda a: a.astype(_BF16)
    xbc_w = SSD_INNER + 2 * SSD_GROUPS * SSD_STATE
    for l in range(DEPTH):
        w = w_in[l]
        ssd_params = (
            bf(w[:, _OFF_Z:_OFF_XBC]),
            bf(w[:, _OFF_XBC:_OFF_DT]),
            bf(_pad_cols(w[:, _OFF_DT:_OFF_GDN_QKV], LANES)),
            ssd_conv_w[l], _row(ssd_conv_b[l]),
            _row(ssd_dt_bias[l], LANES), _row(ssd_a_log[l], LANES),
            _row(jnp.repeat(ssd_d[l], SSD_HEAD_DIM)), _row(ssd_norm_w[l]),
            bf(w_br_ssd[l]),
        )
        y_ssd = _sequence_call(
            _ssd_kernel, "ssd", x, ssd_params, D_MODEL,
            [pltpu.VMEM((SUBLANES, xbc_w), _F32),
             pltpu.VMEM((SSD_GROUPS, SSD_STATE, SSD_INNER // SSD_GROUPS), _F32),
             pltpu.VMEM((TIME_BLOCK, SSD_INNER), _F32)])
        w_ab = jnp.concatenate([_pad_cols(w[:, _OFF_GDN_A:_OFF_GDN_B], LANES),
                                _pad_cols(w[:, _OFF_GDN_B:_OFF_GDN_G], LANES)], axis=1)
        gdn_params = (
            bf(w[:, _OFF_GDN_QKV:_OFF_GDN_A]), bf(w_ab), bf(w[:, _OFF_GDN_G:_OFF_GLA_QKV]),
            gdn_conv_w[l], _row(gdn_a_log[l], LANES), _row(gdn_dt_bias[l], LANES),
            _row(gdn_norm_w[l]), bf(w_br_gdn[l]),
        )
        y_gdn = _sequence_call(
            _gdn_kernel, "gdn", x, gdn_params, D_MODEL,
            [pltpu.VMEM((SUBLANES, 3 * GDN_WIDTH), _F32),
             pltpu.VMEM((GDN_HEADS, GDN_HEAD_DIM, GDN_HEAD_DIM), _F32),
             pltpu.VMEM((TIME_BLOCK, GDN_WIDTH), _F32)])
        w2 = jnp.pad(gla_gate_w2[l], ((0, LANES - GLA_GATE_RANK), (0, 0)))
        gla_params = (
            bf(w[:, _OFF_GLA_QKV:_OFF_GLA_GLR]),
            bf(_pad_cols(w[:, _OFF_GLA_GLR:_OFF_GLA_R], LANES)),
            bf(w[:, _OFF_GLA_R:_OFF_GATES]),
            bf(w2), _row(gla_gate_b[l]), _row(gla_norm_w[l]), bf(w_br_gla[l]),
        )
        y_gla = _sequence_call(
            _gla_kernel, "gla", x, gla_params, D_MODEL,
            [pltpu.VMEM((GLA_HEADS, GLA_KEY_DIM, GLA_VAL_DIM), _F32)])
        x = _merge_call(x, (y_ssd, y_gdn, y_gla),
                        (bf(w[:, _OFF_GATES:]), bf(w_out[l]), _row(ln1_g[l]), _row(ln1_b[l])))
        ffn_params = (bf(ffn_w_up[l]), ffn_conv_w[l], _row(ffn_conv_b[l]), bf(ffn_w_down[l]),
                      _row(ln2_g[l]), _row(ln2_b[l]))
        x = _sequence_call(_ffn_kernel, "ffn", x, ffn_params, D_MODEL,
                           [pltpu.VMEM((SUBLANES, 2 * FFN_DIM), _F32)])
    return x
```

```python
import jax
import jax.numpy as jnp
from jax import lax
from jax.experimental import pallas as pl
from jax.experimental.pallas import tpu as pltpu

_F32 = jnp.float32
_BF16 = jnp.bfloat16

D_MODEL = 1024
DEPTH = 2
SSD_HEADS = 16
SSD_HEAD_DIM = 64
SSD_INNER = SSD_HEADS * SSD_HEAD_DIM
SSD_GROUPS = 2
SSD_STATE = 128
SSD_CONV = 4
GDN_HEADS = 4
GDN_HEAD_DIM = 128
GDN_WIDTH = GDN_HEADS * GDN_HEAD_DIM
GDN_CONV = 4
GDN_CHUNK = 64
GLA_HEADS = 4
GLA_KEY_DIM = 64
GLA_VAL_DIM = 128
GLA_K = GLA_HEADS * GLA_KEY_DIM
GLA_V = GLA_HEADS * GLA_VAL_DIM
GLA_GATE_RANK = 16
GLA_GATE_NORMALIZER = 16.0
GLA_CHUNK = 16
FFN_DIM = 2816
FFN_CONV = 3
N_BRANCH = 3
ALPHA = (2 * DEPTH) ** 0.25
LN_EPS = 1e-5
RMS_EPS = 1e-6

LANES = 128
SUBLANES = 8
TIME_BLOCK = 256
ROWS_PER_STEP = 2
SSD_CHUNK = 128
SUB_ROWS = 256
MERGE_ROWS = 1024
FFN_TIME_BLOCK = 512
VMEM_LIMIT = 56 * 1024 * 1024

_OFF_Z = 0
_OFF_XBC = _OFF_Z + SSD_INNER
_OFF_DT = _OFF_XBC + SSD_INNER + 2 * SSD_GROUPS * SSD_STATE
_OFF_GDN_QKV = _OFF_DT + SSD_HEADS
_OFF_GDN_A = _OFF_GDN_QKV + 3 * GDN_WIDTH
_OFF_GDN_B = _OFF_GDN_A + GDN_HEADS
_OFF_GDN_G = _OFF_GDN_B + GDN_HEADS
_OFF_GLA_QKV = _OFF_GDN_G + GDN_WIDTH
_OFF_GLA_GLR = _OFF_GLA_QKV + 2 * GLA_K + GLA_V
_OFF_GLA_R = _OFF_GLA_GLR + GLA_GATE_RANK
_OFF_GATES = _OFF_GLA_R + GLA_V
_IN_DIM = _OFF_GATES + N_BRANCH * D_MODEL


def _log2(n):
    assert n & (n - 1) == 0
    return n.bit_length() - 1


def _mm(a, b):
    return jnp.dot(a.astype(_BF16), b.astype(_BF16), preferred_element_type=_F32)


def _mm_nt(a, b):
    return lax.dot_general(a.astype(_BF16), b.astype(_BF16), (((1,), (1,)), ((), ())),
                           preferred_element_type=_F32)


def _bmm(a, b):
    return lax.dot_general(a.astype(_BF16), b.astype(_BF16), (((2,), (1,)), ((0,), (0,))),
                           preferred_element_type=_F32)


def _bmm_nt(a, b):
    return lax.dot_general(a.astype(_BF16), b.astype(_BF16), (((2,), (2,)), ((0,), (0,))),
                           preferred_element_type=_F32)


def _silu(v):
    return v * jax.nn.sigmoid(v)


def _softplus(v):
    return jnp.maximum(v, 0.0) + jnp.log1p(jnp.exp(-jnp.abs(v)))


def _split_bf16(v, n):
    parts, rest = [], v
    for i in range(n):
        p = rest.astype(_BF16)
        parts.append(p)
        if i + 1 < n:
            rest = rest - p.astype(_F32)
    return parts


def _chunk_masks(n, chunk):
    shift = _log2(chunk)
    r = lax.broadcasted_iota(jnp.int32, (n, n), 0)
    c = lax.broadcasted_iota(jnp.int32, (n, n), 1)
    same = (r >> shift) == (c >> shift)
    return same, jnp.logical_and(same, c <= r)


def _chunk_cumsum(v, chunk, rows):
    width = v.shape[1]
    same, tri = _chunk_masks(rows, chunk)
    lhs = jnp.concatenate([jnp.where(tri, 1.0, 0.0), jnp.where(same, 1.0, 0.0)], axis=0).astype(_BF16)
    cs, tot = [], []
    for r0 in range(0, v.shape[0], rows):
        rhs = jnp.concatenate(_split_bf16(v[r0:r0 + rows], 3), axis=1)
        r = jnp.dot(lhs, rhs, preferred_element_type=_F32)
        s = r[:, :width] + r[:, width:2 * width] + r[:, 2 * width:]
        cs.append(s[:rows])
        tot.append(s[rows:])
    return jnp.concatenate(cs, axis=0), jnp.concatenate(tot, axis=0)


def _causal_dwconv(pre, prev, w_ref, width):
    n = pre.shape[0]
    ext = jnp.concatenate([prev, pre], axis=0)
    acc = pre * w_ref[width - 1:width, :]
    for k in range(width - 1):
        back = width - 1 - k
        acc = acc + ext[SUBLANES - back:SUBLANES - back + n, :] * w_ref[k:k + 1, :]
    return acc, pre[n - SUBLANES:, :]


def _conv_rows(pre, tail_ref, w_ref, width, rows):
    out = []
    for i, r0 in enumerate(range(0, pre.shape[0], rows)):
        y, tail = _causal_dwconv(pre[r0:r0 + rows], tail_ref[i], w_ref, width)
        tail_ref[i] = tail
        out.append(y)
    return jnp.concatenate(out, axis=0)


def _layer_norm(v, g, b):
    mu = jnp.mean(v, axis=-1, keepdims=True)
    c = v - mu
    var = jnp.mean(c * c, axis=-1, keepdims=True)
    return c * lax.rsqrt(var + LN_EPS) * g + b


def _rms(v):
    return v * lax.rsqrt(jnp.mean(v * v, axis=-1, keepdims=True) + RMS_EPS)


def _l2norm(v):
    return v * lax.rsqrt(jnp.sum(v * v, axis=-1, keepdims=True) + RMS_EPS)


def _unit_lower_inverse(a):
    n = a.shape[-1]
    r = lax.broadcasted_iota(jnp.int32, (n, n), 0)
    c = lax.broadcasted_iota(jnp.int32, (n, n), 1)
    p = -a
    t = jnp.where(r == c, 1.0, 0.0) + p
    for _ in range(_log2(n) - 1):
        p = _bmm(p, p)
        t = t + _bmm(t, p)
    return t


def _ssd_kernel(x_ref, wz_ref, wxbc_ref, wdt_ref, cw_ref, cb_ref, dtb_ref, alog_ref, dskip_ref,
                nw_ref, wbr_ref, o_ref, tail_ref, st_ref, y_ref):
    nr, tb, dm = x_ref.shape
    n = nr * tb
    lc = SSD_CHUNK
    gw = SSD_INNER // SSD_GROUPS

    @pl.when(pl.program_id(1) == 0)
    def _():
        tail_ref[...] = jnp.zeros_like(tail_ref)
        st_ref[...] = jnp.zeros_like(st_ref)

    xb = x_ref[...].reshape(n, dm).astype(_BF16)
    z = jnp.dot(xb, wz_ref[...], preferred_element_type=_F32)
    pre = jnp.dot(xb, wxbc_ref[...], preferred_element_type=_F32)
    dtr = jnp.dot(xb, wdt_ref[...], preferred_element_type=_F32)

    xbc = _silu(_conv_rows(pre, tail_ref, cw_ref, SSD_CONV, tb) + cb_ref[...])
    xs = xbc[:, :SSD_INNER]
    bm = xbc[:, SSD_INNER:SSD_INNER + SSD_GROUPS * SSD_STATE]
    cm = xbc[:, SSD_INNER + SSD_GROUPS * SSD_STATE:]

    dt = _softplus(dtr + dtb_ref[...])
    a = dt * (-jnp.exp(alog_ref[...]))
    a_cs, a_end = _chunk_cumsum(a, lc, tb)
    a_cs_t = a_cs.T
    dt_t = dt.T

    factors = jnp.concatenate([jnp.exp(a_cs), jnp.exp(a_end - a_cs) * dt], axis=0)
    hi = factors.astype(_BF16).astype(_F32)
    flane = lax.broadcasted_iota(jnp.int32, factors.shape, 1)
    packed = jnp.where(flane < SSD_HEADS, hi, factors - hi).astype(_BF16)
    er = lax.broadcasted_iota(jnp.int32, (LANES, SSD_INNER), 0)
    ec = lax.broadcasted_iota(jnp.int32, (LANES, SSD_INNER), 1)
    expand = jnp.where(jnp.logical_and(er < 2 * SSD_HEADS,
                                       (er & (SSD_HEADS - 1)) == (ec >> _log2(SSD_HEAD_DIM))),
                       1.0, 0.0).astype(_BF16)
    fx = jnp.dot(packed, expand, preferred_element_type=_F32)
    dec_in_x, w_st_x = fx[:n], fx[n:]

    lr = lax.broadcasted_iota(jnp.int32, (lc, lc), 0)
    lcol = lax.broadcasted_iota(jnp.int32, (lc, lc), 1)
    causal = lcol <= lr
    lane = lax.broadcasted_iota(jnp.int32, (lc, LANES), 1)

    for c in range(tb // lc):
        for r in range(nr):
            r0 = r * tb + c * lc
            r1 = r0 + lc
            for g in range(SSD_GROUPS):
                cg = cm[r0:r1, g * SSD_STATE:(g + 1) * SSD_STATE]
                bg = bm[r0:r1, g * SSD_STATE:(g + 1) * SSD_STATE]
                cb = _mm_nt(cg, bg)
                st = st_ref[r, g]
                y_inter = _mm(cg, st) * dec_in_x[r0:r1, g * gw:(g + 1) * gw]
                for blk in range(gw // LANES):
                    col0 = g * gw + blk * LANES
                    xs_blk = xs[r0:r1, col0:col0 + LANES]
                    wms = []
                    for half in range(2):
                        h = col0 // SSD_HEAD_DIM + half
                        seg = a_cs[r0:r1, h:h + 1] - a_cs_t[h:h + 1, r0:r1]
                        dec = jnp.where(causal, jnp.exp(jnp.minimum(seg, 0.0)), 0.0)
                        wms.append(cb * dec * dt_t[h:h + 1, r0:r1])
                    rhs = jnp.concatenate([jnp.where(lane < SSD_HEAD_DIM, xs_blk, 0.0),
                                           jnp.where(lane >= SSD_HEAD_DIM, xs_blk, 0.0)], axis=0)
                    y_ref[r0:r1, col0:col0 + LANES] = (
                        y_inter[:, blk * LANES:(blk + 1) * LANES]
                        + _mm(jnp.concatenate(wms, axis=1), rhs))
                xw = xs[r0:r1, g * gw:(g + 1) * gw] * w_st_x[r0:r1, g * gw:(g + 1) * gw]
                st_ref[r, g] = st * dec_in_x[r1 - 1:r1, g * gw:(g + 1) * gw] + _mm(bg.T, xw)

    y = (y_ref[...] + dskip_ref[...] * xs) * _silu(z)
    y = jnp.concatenate([_rms(y[:, g * gw:(g + 1) * gw]) for g in range(SSD_GROUPS)], axis=1)
    o_ref[...] = _mm(y * nw_ref[...], wbr_ref[...]).reshape(nr, tb, dm)


def _gdn_kernel(x_ref, wqkv_ref, wab_ref, wg_ref, cw_ref, alog_ref, dtb_ref, nw_ref, wbr_ref,
                o_ref, tail_ref, s_ref, oacc_ref):
    nr, tb, dm = x_ref.shape
    n = nr * tb
    lc = GDN_CHUNK
    dh = GDN_HEAD_DIM
    nc = tb // lc

    @pl.when(pl.program_id(1) == 0)
    def _():
        tail_ref[...] = jnp.zeros_like(tail_ref)
        s_ref[...] = jnp.zeros_like(s_ref)

    xb = x_ref[...].reshape(n, dm).astype(_BF16)
    pre = jnp.dot(xb, wqkv_ref[...], preferred_element_type=_F32)
    ab = jnp.dot(xb, wab_ref[...], preferred_element_type=_F32)
    gate = jnp.dot(xb, wg_ref[...], preferred_element_type=_F32)
    qkv = _silu(_conv_rows(pre, tail_ref, cw_ref, GDN_CONV, tb))

    beta = jax.nn.sigmoid(ab[:, LANES:])
    gdec = -jnp.exp(alog_ref[...]) * _softplus(ab[:, :LANES] + dtb_ref[...])
    g_cs, g_end = _chunk_cumsum(gdec, lc, tb)
    g_cs_t = g_cs.T
    e_cs = jnp.exp(g_cs)
    e_rem = jnp.exp(g_end - g_cs)
    e_end = jnp.exp(g_end)

    lr = lax.broadcasted_iota(jnp.int32, (lc, lc), 0)
    lcol = lax.broadcasted_iota(jnp.int32, (lc, lc), 1)
    incl = lcol <= lr
    strict = lcol < lr

    heads = []
    for h in range(GDN_HEADS):
        q = _l2norm(qkv[:, h * dh:(h + 1) * dh]) * (dh ** -0.5)
        k = _l2norm(qkv[:, GDN_WIDTH + h * dh:GDN_WIDTH + (h + 1) * dh])
        v = qkv[:, 2 * GDN_WIDTH + h * dh:2 * GDN_WIDTH + (h + 1) * dh]
        bcol = beta[:, h:h + 1]
        kb = k * bcol
        heads.append(dict(q=q, k=k, kb=kb, vb=v * bcol, kbe=kb * e_cs[:, h:h + 1],
                          q_dec=q * e_cs[:, h:h + 1], k_dec=k * e_rem[:, h:h + 1]))
    order = [(r * tb + c * lc, h) for c in range(nc) for r in range(nr) for h in range(GDN_HEADS)]
    per_chunk = nr * GDN_HEADS

    def stack(name):
        return jnp.stack([heads[h][name][r0:r0 + lc] for r0, h in order], axis=0)

    def gamma_of(r0, h):
        seg = g_cs[r0:r0 + lc, h:h + 1] - g_cs_t[h:h + 1, r0:r0 + lc]
        return jnp.where(incl, jnp.exp(jnp.minimum(seg, 0.0)), 0.0)

    k3 = stack("k")
    gamma3 = jnp.stack([gamma_of(r0, h) for r0, h in order], axis=0)
    a3 = jnp.where(strict, _bmm_nt(stack("kb"), k3) * gamma3, 0.0)
    attn3 = jnp.where(incl, _bmm_nt(stack("q"), k3) * gamma3, 0.0)
    uw3 = _bmm(_unit_lower_inverse(a3), jnp.concatenate([stack("vb"), stack("kbe")], axis=2))
    q_dec3 = stack("q_dec")
    k_dec_t3 = jnp.stack([heads[h]["k_dec"][r0:r0 + lc].T for r0, h in order], axis=0)

    s = s_ref[...]
    for c in range(nc):
        b0, b1 = c * per_chunk, (c + 1) * per_chunk
        ws = _bmm(jnp.concatenate([uw3[b0:b1, :, dh:], q_dec3[b0:b1]], axis=1), s)
        v_new = uw3[b0:b1, :, :dh] - ws[:, :lc]
        o_c = ws[:, lc:] + _bmm(attn3[b0:b1], v_new)
        decay = jnp.stack([e_end[r0 + lc - 1:r0 + lc, h:h + 1] for r0, h in order[b0:b1]], axis=0)
        s = s * decay + _bmm(k_dec_t3[b0:b1], v_new)
        for i, (r0, h) in enumerate(order[b0:b1]):
            oacc_ref[r0:r0 + lc, h * dh:(h + 1) * dh] = o_c[i]
    s_ref[...] = s

    o = oacc_ref[...]
    o = jnp.concatenate([_rms(o[:, h * dh:(h + 1) * dh]) * nw_ref[...] for h in range(GDN_HEADS)],
                        axis=1)
    o_ref[...] = _mm(o * _silu(gate), wbr_ref[...]).reshape(nr, tb, dm)


def _gla_kernel(x_ref, wqkv_ref, wglr_ref, wr_ref, w2_ref, gb_ref, nw_ref, wbr_ref,
                o_ref, s_ref):
    nr, tb, dm = x_ref.shape
    n = nr * tb
    lc = GLA_CHUNK
    dk, dv = GLA_KEY_DIM, GLA_VAL_DIM
    nc = tb // lc

    @pl.when(pl.program_id(1) == 0)
    def _():
        s_ref[...] = jnp.zeros_like(s_ref)

    xb = x_ref[...].reshape(n, dm).astype(_BF16)
    qkv = jnp.dot(xb, wqkv_ref[...], preferred_element_type=_F32)
    glr = jnp.dot(xb, wglr_ref[...], preferred_element_type=_F32)
    rgate = jnp.dot(xb, wr_ref[...], preferred_element_type=_F32)
    q = qkv[:, :GLA_K] * (dk ** -0.5)
    k = qkv[:, GLA_K:2 * GLA_K]
    vb = qkv[:, 2 * GLA_K:].astype(_BF16)

    gk = -_softplus(-(_mm(glr, w2_ref[...]) + gb_ref[...])) / GLA_GATE_NORMALIZER
    b_cs, b_end = _chunk_cumsum(gk, lc, tb)
    q_e = q * jnp.exp(b_cs)
    k_e = k * jnp.exp(-b_cs)
    k_d_t = (k * jnp.exp(b_end - b_cs)).T
    d_t = jnp.exp(b_end).T

    _, tri = _chunk_masks(tb, lc)
    lane_k = lax.broadcasted_iota(jnp.int32, (n, GLA_K), 1)
    q_heads = [jnp.where((lane_k >> _log2(dk)) == h, q_e, 0.0) for h in range(GLA_HEADS)]
    urow = lax.broadcasted_iota(jnp.int32, (nc * dk, tb), 0)
    ucol = lax.broadcasted_iota(jnp.int32, (nc * dk, tb), 1)
    own_chunk = (urow >> _log2(dk)) == (ucol >> _log2(lc))
    units = [(r, h) for r in range(nr) for h in range(GLA_HEADS)]

    intra = {}
    for r, h in units:
        rows = slice(r * tb, (r + 1) * tb)
        scores = jnp.where(tri, _mm_nt(q_heads[h][rows], k_e[rows]), 0.0)
        intra[r, h] = _mm(scores, vb[rows, h * dv:(h + 1) * dv])

    upd = {}
    for r, h in units:
        kt = k_d_t[h * dk:(h + 1) * dk, r * tb:(r + 1) * tb]
        upd[r, h] = _mm(jnp.where(own_chunk, jnp.tile(kt, (nc, 1)), 0.0),
                        vb[r * tb:(r + 1) * tb, h * dv:(h + 1) * dv])

    s = {u: s_ref[u[0] * GLA_HEADS + u[1]] for u in units}
    entering, lhs = [], []
    for c in range(nc):
        for r in range(nr):
            entering.append(jnp.concatenate([s[r, h] for h in range(GLA_HEADS)], axis=0))
            t0 = r * tb + c * lc
            lhs.append(jnp.concatenate([qh[t0:t0 + lc] for qh in q_heads], axis=0))
            for h in range(GLA_HEADS):
                s[r, h] = (s[r, h] * d_t[h * dk:(h + 1) * dk, t0:t0 + 1]
                           + upd[r, h][c * dk:(c + 1) * dk])
    for r, h in units:
        s_ref[r * GLA_HEADS + h] = s[r, h]
    res3 = _bmm(jnp.stack(lhs, axis=0), jnp.stack(entering, axis=0))

    o = []
    for h in range(GLA_HEADS):
        cols = []
        for r in range(nr):
            inter = jnp.concatenate([res3[c * nr + r, h * lc:(h + 1) * lc] for c in range(nc)],
                                    axis=0)
            cols.append(intra[r, h] + inter)
        o.append(_rms(jnp.concatenate(cols, axis=0)) * nw_ref[...])
    o = jnp.concatenate(o, axis=1)
    o_ref[...] = _mm(o * _silu(rgate), wbr_ref[...]).reshape(nr, tb, dm)


def _merge_kernel(x_ref, y0_ref, y1_ref, y2_ref, wg_ref, wo_ref, g_ref, b_ref, o_ref):
    for r0 in range(0, x_ref.shape[0], SUB_ROWS):
        rows = slice(r0, r0 + SUB_ROWS)
        x = x_ref[rows, :]
        logits = jnp.dot(x.astype(_BF16), wg_ref[...], preferred_element_type=_F32)
        mix = (jax.nn.sigmoid(logits[:, :D_MODEL]) * y0_ref[rows, :]
               + jax.nn.sigmoid(logits[:, D_MODEL:2 * D_MODEL]) * y1_ref[rows, :]
               + jax.nn.sigmoid(logits[:, 2 * D_MODEL:]) * y2_ref[rows, :])
        o_ref[rows, :] = _layer_norm(ALPHA * x + _mm(mix, wo_ref[...]), g_ref[...], b_ref[...])


def _ffn_kernel(x_ref, wup_ref, cw_ref, cb_ref, wdn_ref, g_ref, b_ref, o_ref, tail_ref):
    @pl.when(pl.program_id(1) == 0)
    def _():
        tail_ref[...] = jnp.zeros_like(tail_ref)

    tail = tail_ref[...]
    for r0 in range(0, x_ref.shape[0], SUB_ROWS):
        rows = slice(r0, r0 + SUB_ROWS)
        x = x_ref[rows, :]
        pre = jnp.dot(x.astype(_BF16), wup_ref[...], preferred_element_type=_F32)
        hid, tail = _causal_dwconv(pre, tail, cw_ref, FFN_CONV)
        hid = hid + cb_ref[...]
        act = _silu(hid[:, :FFN_DIM]) * hid[:, FFN_DIM:]
        o_ref[rows, :] = _layer_norm(ALPHA * x + _mm(act, wdn_ref[...]), g_ref[...], b_ref[...])
    tail_ref[...] = tail


def _resident(a):
    return pl.BlockSpec(a.shape, lambda *_: (0,) * a.ndim)


def _mixer_call(body, name, x, params, scratch):
    bsz, seq, dm = x.shape
    nr, tb = ROWS_PER_STEP, TIME_BLOCK
    assert bsz % nr == 0 and seq % tb == 0 and dm == D_MODEL
    blk = pl.BlockSpec((nr, tb, dm), lambda b, j: (b, j, 0))
    return pl.pallas_call(
        body,
        grid=(bsz // nr, seq // tb),
        in_specs=[blk] + [_resident(p) for p in params],
        out_specs=blk,
        out_shape=jax.ShapeDtypeStruct((bsz, seq, dm), _F32),
        scratch_shapes=scratch,
        compiler_params=pltpu.CompilerParams(dimension_semantics=("parallel", "arbitrary"),
                                             vmem_limit_bytes=VMEM_LIMIT),
        name=name,
    )(x, *params)


def _ffn_call(x, params):
    bsz, seq, dm = x.shape
    tb = FFN_TIME_BLOCK
    assert seq % tb == 0 and tb % SUB_ROWS == 0
    blk = pl.BlockSpec((None, tb, dm), lambda b, j: (b, j, 0))
    return pl.pallas_call(
        _ffn_kernel,
        grid=(bsz, seq // tb),
        in_specs=[blk] + [_resident(p) for p in params],
        out_specs=blk,
        out_shape=jax.ShapeDtypeStruct((bsz, seq, dm), _F32),
        scratch_shapes=[pltpu.VMEM((SUBLANES, 2 * FFN_DIM), _F32)],
        compiler_params=pltpu.CompilerParams(dimension_semantics=("parallel", "arbitrary"),
                                             vmem_limit_bytes=VMEM_LIMIT),
        name="ffn",
    )(x, *params)


def _merge_call(x, ys, params):
    bsz, seq, dm = x.shape
    rows = bsz * seq
    rb = MERGE_ROWS
    assert rows % rb == 0 and rb % SUB_ROWS == 0
    row_spec = pl.BlockSpec((rb, dm), lambda i: (i, 0))
    out = pl.pallas_call(
        _merge_kernel,
        grid=(rows // rb,),
        in_specs=[row_spec] * 4 + [_resident(p) for p in params],
        out_specs=row_spec,
        out_shape=jax.ShapeDtypeStruct((rows, dm), _F32),
        compiler_params=pltpu.CompilerParams(dimension_semantics=("parallel",),
                                             vmem_limit_bytes=VMEM_LIMIT),
        name="merge",
    )(x.reshape(rows, dm), *[y.reshape(rows, dm) for y in ys], *params)
    return out.reshape(bsz, seq, dm)


def _pad_cols(a, width):
    return jnp.pad(a, ((0, 0), (0, width - a.shape[-1])))


def _row(v, width=None):
    v = v.reshape(1, -1).astype(_F32)
    return v if width is None else _pad_cols(v, width)


def kernel(x, w_in, ssd_conv_w, ssd_conv_b, ssd_dt_bias, ssd_a_log, ssd_d, ssd_norm_w, gdn_conv_w, gdn_a_log, gdn_dt_bias, gdn_norm_w, gla_gate_w2, gla_gate_b, gla_norm_w, w_br_ssd, w_br_gdn, w_br_gla, w_out, ln1_g, ln1_b, ffn_w_up, ffn_conv_w, ffn_conv_b, ffn_w_down, ln2_g, ln2_b):
    assert w_in.shape == (DEPTH, D_MODEL, _IN_DIM)
    nr = ROWS_PER_STEP
    xbc_w = SSD_INNER + 2 * SSD_GROUPS * SSD_STATE
    twice = lambda v: jnp.concatenate([v, v], axis=-1)
    w_in_b, w_br_ssd_b, w_br_gdn_b, w_br_gla_b, w_out_b, w_up_b, w_dn_b, w2_b = (
        a.astype(_BF16) for a in (w_in, w_br_ssd, w_br_gdn, w_br_gla, w_out, ffn_w_up, ffn_w_down,
                                  gla_gate_w2))
    for l in range(DEPTH):
        w = w_in_b[l]
        ssd_params = (
            w[:, _OFF_Z:_OFF_XBC],
            w[:, _OFF_XBC:_OFF_DT],
            _pad_cols(twice(w[:, _OFF_DT:_OFF_GDN_QKV]), LANES),
            ssd_conv_w[l], _row(ssd_conv_b[l]),
            _row(twice(ssd_dt_bias[l]), LANES), _row(twice(ssd_a_log[l]), LANES),
            _row(jnp.repeat(ssd_d[l], SSD_HEAD_DIM)), _row(ssd_norm_w[l]),
            w_br_ssd_b[l],
        )
        y_ssd = _mixer_call(
            _ssd_kernel, "ssd", x, ssd_params,
            [pltpu.VMEM((nr, SUBLANES, xbc_w), _F32),
             pltpu.VMEM((nr, SSD_GROUPS, SSD_STATE, SSD_INNER // SSD_GROUPS), _F32),
             pltpu.VMEM((nr * TIME_BLOCK, SSD_INNER), _F32)])
        w_ab = jnp.concatenate([_pad_cols(w[:, _OFF_GDN_A:_OFF_GDN_B], LANES),
                                _pad_cols(w[:, _OFF_GDN_B:_OFF_GDN_G], LANES)], axis=1)
        gdn_params = (
            w[:, _OFF_GDN_QKV:_OFF_GDN_A], w_ab, w[:, _OFF_GDN_G:_OFF_GLA_QKV],
            gdn_conv_w[l], _row(gdn_a_log[l], LANES), _row(gdn_dt_bias[l], LANES),
            _row(gdn_norm_w[l]), w_br_gdn_b[l],
        )
        y_gdn = _mixer_call(
            _gdn_kernel, "gdn", x, gdn_params,
            [pltpu.VMEM((nr, SUBLANES, 3 * GDN_WIDTH), _F32),
             pltpu.VMEM((nr * GDN_HEADS, GDN_HEAD_DIM, GDN_HEAD_DIM), _F32),
             pltpu.VMEM((nr * TIME_BLOCK, GDN_WIDTH), _F32)])
        gla_params = (
            w[:, _OFF_GLA_QKV:_OFF_GLA_GLR],
            _pad_cols(w[:, _OFF_GLA_GLR:_OFF_GLA_R], LANES),
            w[:, _OFF_GLA_R:_OFF_GATES],
            jnp.pad(w2_b[l], ((0, LANES - GLA_GATE_RANK), (0, 0))),
            _row(gla_gate_b[l]), _row(gla_norm_w[l]), w_br_gla_b[l],
        )
        y_gla = _mixer_call(
            _gla_kernel, "gla", x, gla_params,
            [pltpu.VMEM((nr * GLA_HEADS, GLA_KEY_DIM, GLA_VAL_DIM), _F32)])
        x = _merge_call(x, (y_ssd, y_gdn, y_gla),
                        (w[:, _OFF_GATES:], w_out_b[l], _row(ln1_g[l]), _row(ln1_b[l])))
        x = _ffn_call(x, (w_up_b[l], ffn_conv_w[l], _row(ffn_conv_b[l]), w_dn_b[l],
                          _row(ln2_g[l]), _row(ln2_b[l])))
    return x
```
